```python
import jax, jax.numpy as jnp
from jax import lax
import numpy as np

D_MODEL = 4096
BATCH = 1
SEQ = 16384
DEPTH = 2

RMS_EPS = 1e-6
CONV_W = 4
LRU_WIDTH = D_MODEL
LRU_BLOCKS = 16
LRU_BLOCK = LRU_WIDTH // LRU_BLOCKS
LRU_C = 8.0
SSD_EXPAND = 2
SSD_INNER = SSD_EXPAND * D_MODEL
SSD_HEADDIM = 64
SSD_HEADS = SSD_INNER // SSD_HEADDIM
SSD_STATE = 128
SSD_GROUPS = 8
SSD_HPG = SSD_HEADS // SSD_GROUPS
SSD_CHUNK = 128
SSD_BC = SSD_GROUPS * SSD_STATE
SSD_CONV_DIM = SSD_INNER + 2 * SSD_BC
SSD_PROJ = 2 * SSD_INNER + 2 * SSD_BC + SSD_HEADS
FFN_DENSE = 14336
N_EXPERTS = 8
TOP_K = 2
FFN_EXPERT = 4096

kernel_name = "hybrid_rglru_ssd_moe_trunk"


def rmsnorm(x, w):
    xf = x.astype(jnp.float32)
    var = jnp.mean(xf * xf, axis=-1, keepdims=True)
    return (xf * lax.rsqrt(var + RMS_EPS)).astype(x.dtype) * w


def causal_dwconv(x, w, b):
    S = x.shape[1]
    xp = jnp.pad(x, ((0, 0), (CONV_W - 1, 0), (0, 0)))
    y = b
    for k in range(CONV_W):
        y = y + w[k] * xp[:, k:k + S]
    return y


def linear_recurrence(a, b):
    def step(h, ab):
        a_t, b_t = ab
        h = a_t * h + b_t
        return h, h
    h0 = jnp.zeros_like(b[:, 0])
    _, hs = lax.scan(step, h0, (jnp.swapaxes(a, 0, 1), jnp.swapaxes(b, 0, 1)))
    return jnp.swapaxes(hs, 0, 1)


def rglru_mixer(x, w_in, conv_w, conv_b, w_ra, b_ra, w_ri, b_ri, lru_lambda, w_out):
    Bsz, S, _ = x.shape
    gx = x @ w_in
    gate, u = jnp.split(gx, 2, axis=-1)
    u = causal_dwconv(u, conv_w, conv_b)
    ub = u.reshape(Bsz, S, LRU_BLOCKS, LRU_BLOCK)
    r = jax.nn.sigmoid(jnp.einsum('bshi,hij->bshj', ub, w_ra) + b_ra).reshape(Bsz, S, LRU_WIDTH)
    i = jax.nn.sigmoid(jnp.einsum('bshi,hij->bshj', ub, w_ri) + b_ri).reshape(Bsz, S, LRU_WIDTH)
    log_a = (-LRU_C * r.astype(jnp.float32)) * jax.nn.softplus(-lru_lambda.astype(jnp.float32))
    a = jnp.exp(log_a)
    mult = jnp.sqrt(-jnp.expm1(2.0 * log_a))
    h = linear_recurrence(a, mult * (i * u).astype(jnp.float32))
    y = h.astype(x.dtype) * jax.nn.gelu(gate)
    return y @ w_out


def ssd_chunked_scan(xdt, loga, Bm, Cm):
    Bsz, S, G, E, P = xdt.shape
    N = Bm.shape[-1]
    nc = S // SSD_CHUNK

    def to_chunks(t):
        return jnp.moveaxis(t.reshape((Bsz, nc, SSD_CHUNK) + t.shape[2:]), 1, 0)

    causal = jnp.tril(jnp.ones((SSD_CHUNK, SSD_CHUNK), dtype=bool))[None, :, :, None, None]

    def step(state, inp):
        xc, ac, bc, cc = inp
        cum = jnp.cumsum(ac, axis=1)
        seg = cum[:, :, None] - cum[:, None, :]
        decay = jnp.exp(jnp.where(causal, seg, -jnp.inf))
        cb = jnp.einsum('blgn,bsgn->blsg', cc, bc)
        y_diag = jnp.einsum('blsg,blsge,bsgep->blgep', cb, decay, xc)
        y_off = jnp.einsum('blgn,bgepn->blgep', cc, state) * jnp.exp(cum)[..., None]
        decay_end = jnp.exp(cum[:, -1:] - cum)
        new_state = state * jnp.exp(cum[:, -1])[..., None, None] + \
            jnp.einsum('bsgn,bsge,bsgep->bgepn', bc, decay_end, xc)
        return new_state, y_diag + y_off

    state0 = jnp.zeros((Bsz, G, E, P, N), jnp.float32)
    _, ys = lax.scan(step, state0, (to_chunks(xdt), to_chunks(loga), to_chunks(Bm), to_chunks(Cm)))
    return jnp.moveaxis(ys, 0, 1).reshape(Bsz, S, G, E, P)


def ssd_mixer(x, w_in, conv_w, conv_b, dt_bias, a_log, d_skip, norm_w, w_out):
    Bsz, S, _ = x.shape
    zxbcdt = x @ w_in
    z, xbc, dt = jnp.split(zxbcdt, [SSD_INNER, SSD_INNER + SSD_CONV_DIM], axis=-1)
    xbc = jax.nn.silu(causal_dwconv(xbc, conv_w, conv_b))
    xs, Bm, Cm = jnp.split(xbc, [SSD_INNER, SSD_INNER + SSD_BC], axis=-1)
    xs = xs.reshape(Bsz, S, SSD_GROUPS, SSD_HPG, SSD_HEADDIM).astype(jnp.float32)
    Bm = Bm.reshape(Bsz, S, SSD_GROUPS, SSD_STATE).astype(jnp.float32)
    Cm = Cm.reshape(Bsz, S, SSD_GROUPS, SSD_STATE).astype(jnp.float32)
    dt = jax.nn.softplus(dt.astype(jnp.float32) + dt_bias.astype(jnp.float32))
    dt = dt.reshape(Bsz, S, SSD_GROUPS, SSD_HPG)
    A = -jnp.exp(a_log.astype(jnp.float32)).reshape(SSD_GROUPS, SSD_HPG)
    y = ssd_chunked_scan(xs * dt[..., None], dt * A, Bm, Cm)
    y = y + d_skip.astype(jnp.float32).reshape(SSD_GROUPS, SSD_HPG)[:, :, None] * xs
    y = y.reshape(Bsz, S, SSD_INNER)
    y = rmsnorm(y * jax.nn.silu(z.astype(jnp.float32)), norm_w)
    return y.astype(x.dtype) @ w_out


def dense_swiglu(x, w_gate, w_up, w_down):
    return (jax.nn.silu(x @ w_gate) * (x @ w_up)) @ w_down


def moe_swiglu(x, w_router, w_gate, w_up, w_down):
    logits = (x @ w_router).astype(jnp.float32)
    top_val, top_idx = lax.top_k(logits, TOP_K)
    gates = jax.nn.softmax(top_val, axis=-1)
    combine = jnp.sum(jax.nn.one_hot(top_idx, N_EXPERTS, dtype=jnp.float32) * gates[..., None], axis=-2)
    out = jnp.zeros_like(x)
    for e in range(N_EXPERTS):
        h = jax.nn.silu(x @ w_gate[e]) * (x @ w_up[e])
        out = out + combine[..., e:e + 1].astype(x.dtype) * (h @ w_down[e])
    return out


def setup_inputs(seed: int = 0) -> dict:
    key = jax.random.key(seed)
    ks = iter(jax.random.split(key, 40))
    f32 = jnp.float32

    def nrm(shape, scale):
        return jax.random.normal(next(ks), shape, f32) * scale

    def gain(shape):
        return 1.0 + 0.02 * jax.random.normal(next(ks), shape, f32)

    D = D_MODEL
    inp = {}
    inp['x'] = jax.random.normal(next(ks), (BATCH, SEQ, D), f32)
    inp['l0_mix_norm'] = gain((D,))
    inp['l0_lru_w_in'] = nrm((D, 2 * LRU_WIDTH), D ** -0.5)
    inp['l0_lru_conv_w'] = nrm((CONV_W, LRU_WIDTH), CONV_W ** -0.5)
    inp['l0_lru_conv_b'] = nrm((LRU_WIDTH,), 0.02)
    inp['l0_lru_w_ra'] = nrm((LRU_BLOCKS, LRU_BLOCK, LRU_BLOCK), LRU_BLOCK ** -0.5)
    inp['l0_lru_b_ra'] = nrm((LRU_BLOCKS, LRU_BLOCK), 0.02)
    inp['l0_lru_w_ri'] = nrm((LRU_BLOCKS, LRU_BLOCK, LRU_BLOCK), LRU_BLOCK ** -0.5)
    inp['l0_lru_b_ri'] = nrm((LRU_BLOCKS, LRU_BLOCK), 0.02)
    a_target = jax.random.uniform(next(ks), (LRU_WIDTH,), f32, 0.9, 0.999)
    log_sig = jnp.log(a_target) / LRU_C
    inp['l0_lru_lambda'] = -jnp.log(jnp.expm1(-log_sig))
    inp['l0_lru_w_out'] = nrm((LRU_WIDTH, D), LRU_WIDTH ** -0.5)
    inp['l0_ffn_norm'] = gain((D,))
    inp['l0_ffn_w_gate'] = nrm((D, FFN_DENSE), D ** -0.5)
    inp['l0_ffn_w_up'] = nrm((D, FFN_DENSE), D ** -0.5)
    inp['l0_ffn_w_down'] = nrm((FFN_DENSE, D), FFN_DENSE ** -0.5)
    inp['l1_mix_norm'] = gain((D,))
    inp['l1_ssd_w_in'] = nrm((D, SSD_PROJ), D ** -0.5)
    inp['l1_ssd_conv_w'] = nrm((CONV_W, SSD_CONV_DIM), CONV_W ** -0.5)
    inp['l1_ssd_conv_b'] = nrm((SSD_CONV_DIM,), 0.02)
    dt0 = jnp.exp(jax.random.uniform(next(ks), (SSD_HEADS,), f32, float(np.log(1e-3)), float(np.log(1e-1))))
    inp['l1_ssd_dt_bias'] = dt0 + jnp.log(-jnp.expm1(-dt0))
    inp['l1_ssd_a_log'] = jnp.log(jax.random.uniform(next(ks), (SSD_HEADS,), f32, 1.0, 16.0))
    inp['l1_ssd_d'] = gain((SSD_HEADS,))
    inp['l1_ssd_norm'] = gain((SSD_INNER,))
    inp['l1_ssd_w_out'] = nrm((SSD_INNER, D), SSD_INNER ** -0.5)
    inp['l1_ffn_norm'] = gain((D,))
    inp['l1_moe_router'] = nrm((D, N_EXPERTS), D ** -0.5)
    inp['l1_moe_w_gate'] = nrm((N_EXPERTS, D, FFN_EXPERT), D ** -0.5)
    inp['l1_moe_w_up'] = nrm((N_EXPERTS, D, FFN_EXPERT), D ** -0.5)
    inp['l1_moe_w_down'] = nrm((N_EXPERTS, FFN_EXPERT, D), FFN_EXPERT ** -0.5)
    inp['final_norm'] = gain((D,))
    return inp


def reference(x,
              l0_mix_norm, l0_lru_w_in, l0_lru_conv_w, l0_lru_conv_b, l0_lru_w_ra, l0_lru_b_ra,
              l0_lru_w_ri, l0_lru_b_ri, l0_lru_lambda, l0_lru_w_out,
              l0_ffn_norm, l0_ffn_w_gate, l0_ffn_w_up, l0_ffn_w_down,
              l1_mix_norm, l1_ssd_w_in, l1_ssd_conv_w, l1_ssd_conv_b, l1_ssd_dt_bias, l1_ssd_a_log,
              l1_ssd_d, l1_ssd_norm, l1_ssd_w_out,
              l1_ffn_norm, l1_moe_router, l1_moe_w_gate, l1_moe_w_up, l1_moe_w_down,
              final_norm):
    layers = [
        (l0_mix_norm, rglru_mixer,
         (l0_lru_w_in, l0_lru_conv_w, l0_lru_conv_b, l0_lru_w_ra, l0_lru_b_ra,
          l0_lru_w_ri, l0_lru_b_ri, l0_lru_lambda, l0_lru_w_out),
         l0_ffn_norm, dense_swiglu, (l0_ffn_w_gate, l0_ffn_w_up, l0_ffn_w_down)),
        (l1_mix_norm, ssd_mixer,
         (l1_ssd_w_in, l1_ssd_conv_w, l1_ssd_conv_b, l1_ssd_dt_bias, l1_ssd_a_log,
          l1_ssd_d, l1_ssd_norm, l1_ssd_w_out),
         l1_ffn_norm, moe_swiglu, (l1_moe_router, l1_moe_w_gate, l1_moe_w_up, l1_moe_w_down)),
    ]
    for i in range(DEPTH):
        mix_norm, mixer, mix_args, ffn_norm, ffn, ffn_args = layers[i]
        x = x + mixer(rmsnorm(x, mix_norm), *mix_args)
        x = x + ffn(rmsnorm(x, ffn_norm), *ffn_args)
    return rmsnorm(x, final_norm)
```

```python
import functools

import jax
import jax.numpy as jnp
from jax import lax
from jax.experimental import pallas as pl
from jax.experimental.pallas import tpu as pltpu

F32 = jnp.float32
BF16 = jnp.bfloat16

RMS_EPS = 1e-6
CONV_W = 4
LRU_C = 8.0
SSD_HEADDIM = 64
SSD_STATE = 128
SSD_CHUNK = 128
TOP_K = 2

V7X_LANES = 128
V7X_SUBLANES = 8
V7X_VMEM_LIMIT = 56 * 1024 * 1024
MM_TILE = 1024
MM_TK = 2048
HALO = 8


def _params(*sem):
    return pltpu.CompilerParams(dimension_semantics=sem, vmem_limit_bytes=V7X_VMEM_LIMIT)


def _tile(n, pref):
    if n <= pref:
        return n
    t = pref - pref % V7X_LANES
    while n % t:
        t -= V7X_LANES
    assert t > 0, (n, pref)
    return t


def _norm_kernel(x_ref, w_ref, o_ref):
    x = x_ref[...]
    var = jnp.mean(x * x, axis=-1, keepdims=True)
    o_ref[...] = (x * lax.rsqrt(var + RMS_EPS) * w_ref[...]).astype(o_ref.dtype)


def _rmsnorm(x, w, out_dtype):
    s, d = x.shape
    tm = _tile(s, 512)
    return pl.pallas_call(
        _norm_kernel,
        grid=(s // tm,),
        in_specs=[pl.BlockSpec((tm, d), lambda i: (i, 0)), pl.BlockSpec((1, d), lambda i: (0, 0))],
        out_specs=pl.BlockSpec((tm, d), lambda i: (i, 0)),
        out_shape=jax.ShapeDtypeStruct((s, d), out_dtype),
        compiler_params=_params("parallel"),
        name="rmsnorm",
    )(x, w.reshape(1, d))


def _mm_kernel(*refs, nk, has_res):
    if has_res:
        a_ref, w_ref, r_ref, o_ref = refs[:4]
        scratch = refs[4:]
    else:
        a_ref, w_ref, o_ref = refs[:3]
        r_ref = None
        scratch = refs[3:]

    def finish(acc):
        if has_res:
            acc = acc + r_ref[...]
        o_ref[...] = acc.astype(o_ref.dtype)

    part = jnp.dot(a_ref[...], w_ref[...], preferred_element_type=F32)
    if nk == 1:
        finish(part)
        return
    acc_ref, = scratch
    k = pl.program_id(2)

    @pl.when(k == 0)
    def _():
        acc_ref[...] = part

    @pl.when(k > 0)
    def _():
        acc_ref[...] += part

    @pl.when(k == nk - 1)
    def _():
        finish(acc_ref[...])


def _matmul(a, w, out_dtype, res=None, tm=MM_TILE, tn=MM_TILE, name="matmul"):
    m, kdim = a.shape
    n = w.shape[1]
    tm = _tile(m, tm)
    tn = _tile(n, tn)
    tk = kdim if kdim <= 2 * MM_TK else _tile(kdim, MM_TK)
    nk = kdim // tk
    in_specs = [pl.BlockSpec((tm, tk), lambda i, j, k: (i, k)), pl.BlockSpec((tk, tn), lambda i, j, k: (k, j))]
    args = [a, w]
    if res is not None:
        in_specs.append(pl.BlockSpec((tm, tn), lambda i, j, k: (i, j)))
        args.append(res)
    return pl.pallas_call(
        functools.partial(_mm_kernel, nk=nk, has_res=res is not None),
        grid=(m // tm, n // tn, nk),
        in_specs=in_specs,
        out_specs=pl.BlockSpec((tm, tn), lambda i, j, k: (i, j)),
        out_shape=jax.ShapeDtypeStruct((m, n), out_dtype),
        scratch_shapes=[pltpu.VMEM((tm, tn), F32)] if nk > 1 else [],
        compiler_params=_params("parallel", "parallel", "arbitrary"),
        name=name,
    )(*args)


def _swiglu_kernel(a_ref, wg_ref, wu_ref, o_ref):
    a = a_ref[...]
    g = jnp.dot(a, wg_ref[...], preferred_element_type=F32)
    u = jnp.dot(a, wu_ref[...], preferred_element_type=F32)
    o_ref[...] = (g * jax.nn.sigmoid(g) * u).astype(o_ref.dtype)


def _swiglu_up(a, wg, wu):
    m, kdim = a.shape
    n = wg.shape[1]
    tm = _tile(m, MM_TILE)
    tn = _tile(n, MM_TILE // 2)
    return pl.pallas_call(
        _swiglu_kernel,
        grid=(m // tm, n // tn),
        in_specs=[
            pl.BlockSpec((tm, kdim), lambda i, j: (i, 0)),
            pl.BlockSpec((kdim, tn), lambda i, j: (0, j)),
            pl.BlockSpec((kdim, tn), lambda i, j: (0, j)),
        ],
        out_specs=pl.BlockSpec((tm, tn), lambda i, j: (i, j)),
        out_shape=jax.ShapeDtypeStruct((m, n), BF16),
        compiler_params=_params("parallel", "parallel"),
        name="swiglu_up",
    )(a, wg, wu)


def _lru_kernel(gate_ref, u_ref, cw_ref, cb_ref, wra_ref, bra_ref, wri_ref, bri_ref, sp_ref, o_ref,
                ubuf, abuf, bbuf, hcar, *, tt, nblk, blk):
    t = pl.program_id(0)

    @pl.when(t == 0)
    def _():
        ubuf[0:HALO, :] = jnp.zeros((HALO, ubuf.shape[1]), F32)
        hcar[...] = jnp.zeros_like(hcar)

    ubuf[HALO:HALO + tt, :] = u_ref[...].astype(F32)
    uc = cb_ref[...]
    for k in range(CONV_W):
        off = HALO - (CONV_W - 1) + k
        uc = uc + cw_ref[k:k + 1, :] * ubuf[off:off + tt, :]
    ubuf[0:HALO, :] = ubuf[tt:tt + HALO, :]

    ucb = uc.astype(BF16)
    for b in range(nblk):
        sl = slice(b * blk, (b + 1) * blk)
        ub = ucb[:, sl]
        r = jax.nn.sigmoid(jnp.dot(ub, wra_ref[b], preferred_element_type=F32) + bra_ref[:, sl])
        i = jax.nn.sigmoid(jnp.dot(ub, wri_ref[b], preferred_element_type=F32) + bri_ref[:, sl])
        log_a = (-LRU_C * r) * sp_ref[:, sl]
        a = jnp.exp(log_a)
        mult = jnp.sqrt(1.0 - jnp.exp(2.0 * log_a))
        abuf[:, sl] = a
        bbuf[:, sl] = mult * (i * uc[:, sl])

    def group(g, h):
        base = pl.multiple_of(g * V7X_SUBLANES, V7X_SUBLANES)
        for r in range(V7X_SUBLANES):
            h = abuf[pl.ds(base + r, 1), :] * h + bbuf[pl.ds(base + r, 1), :]
            bbuf[pl.ds(base + r, 1), :] = h
        return h

    hcar[...] = lax.fori_loop(0, tt // V7X_SUBLANES, group, hcar[...])
    gate = gate_ref[...].astype(F32)
    o_ref[...] = (bbuf[...] * jax.nn.gelu(gate)).astype(o_ref.dtype)


def _lru_core(gx, conv_w, conv_b, w_ra, b_ra, w_ri, b_ri, lam):
    s, w2 = gx.shape
    w = w2 // 2
    nblk, blk, _ = w_ra.shape
    tt = _tile(s, 256)
    sp = jax.nn.softplus(-lam.astype(F32)).reshape(1, w)
    row = lambda v: v.astype(F32).reshape(1, w)
    const2 = lambda shape: pl.BlockSpec(shape, lambda t: (0, 0))
    const3 = lambda shape: pl.BlockSpec(shape, lambda t: (0, 0, 0))
    return pl.pallas_call(
        functools.partial(_lru_kernel, tt=tt, nblk=nblk, blk=blk),
        grid=(s // tt,),
        in_specs=[
            pl.BlockSpec((tt, w), lambda t: (t, 0)),
            pl.BlockSpec((tt, w), lambda t: (t, 1)),
            const2((CONV_W, w)), const2((1, w)),
            const3((nblk, blk, blk)), const2((1, w)),
            const3((nblk, blk, blk)), const2((1, w)),
            const2((1, w)),
        ],
        out_specs=pl.BlockSpec((tt, w), lambda t: (t, 0)),
        out_shape=jax.ShapeDtypeStruct((s, w), BF16),
        scratch_shapes=[
            pltpu.VMEM((tt + HALO, w), F32),
            pltpu.VMEM((tt, w), F32),
            pltpu.VMEM((tt, w), F32),
            pltpu.VMEM((1, w), F32),
        ],
        compiler_params=_params("arbitrary"),
        name="lru_core",
    )(gx, gx, conv_w.astype(F32), row(conv_b), w_ra.astype(BF16), row(b_ra), w_ri.astype(BF16), row(b_ri), sp)


def _ssd_kernel(z_ref, x_ref, b_ref, c_ref, dt_ref, cwx_ref, cbx_ref, cwb_ref, cbb_ref, cwc_ref, cbc_ref,
                dtb_ref, alog_ref, dfull_ref, nw_ref, o_ref,
                xbuf, bbuf, cbuf, xs, bm, cm, dts, cums, state, ybuf, *, q, ngroups, hpg):
    c = pl.program_id(0)
    n = SSD_STATE
    gw = hpg * SSD_HEADDIM

    @pl.when(c == 0)
    def _():
        xbuf[0:HALO, :] = jnp.zeros((HALO, xbuf.shape[1]), F32)
        bbuf[0:HALO, :] = jnp.zeros((HALO, bbuf.shape[1]), F32)
        cbuf[0:HALO, :] = jnp.zeros((HALO, cbuf.shape[1]), F32)
        state[...] = jnp.zeros_like(state)

    def conv_silu(src_ref, buf, cw_ref, cb_ref, dst):
        buf[HALO:HALO + q, :] = src_ref[...].astype(F32)
        acc = cb_ref[...]
        for k in range(CONV_W):
            off = HALO - (CONV_W - 1) + k
            acc = acc + cw_ref[k:k + 1, :] * buf[off:off + q, :]
        buf[0:HALO, :] = buf[q:q + HALO, :]
        dst[...] = acc * jax.nn.sigmoid(acc)

    conv_silu(x_ref, xbuf, cwx_ref, cbx_ref, xs)
    conv_silu(b_ref, bbuf, cwb_ref, cbb_ref, bm)
    conv_silu(c_ref, cbuf, cwc_ref, cbc_ref, cm)

    dt = jax.nn.softplus(dt_ref[...] + dtb_ref[...])
    loga = dt * (-jnp.exp(alog_ref[...]))
    rows = lax.broadcasted_iota(jnp.int32, (q, q), 0)
    cols = lax.broadcasted_iota(jnp.int32, (q, q), 1)
    causal = rows >= cols
    ltri = jnp.where(causal, 1.0, 0.0).astype(F32)
    dts[...] = dt
    cums[...] = jnp.dot(ltri, loga, preferred_element_type=F32, precision=lax.Precision.HIGHEST)

    lane = lax.broadcasted_iota(jnp.int32, (1, V7X_LANES), 1)
    lo = lane < SSD_HEADDIM
    neg_inf = jnp.float32(-jnp.inf)

    def group(g, carry):
        goff = pl.multiple_of(g * V7X_LANES, V7X_LANES)
        bg = bm[:, pl.ds(goff, n)]
        cg = cm[:, pl.ds(goff, n)]
        dtg = dts[:, pl.ds(goff, V7X_LANES)]
        cumg = cums[:, pl.ds(goff, V7X_LANES)]
        cb = lax.dot_general(cg.astype(BF16), bg.astype(BF16), (((1,), (1,)), ((), ())),
                             preferred_element_type=F32)
        bt = bg.T
        cum_t = cumg.T
        dt_t = dtg.T
        for pair in range(hpg // 2):
            ch = pl.multiple_of(g * gw + pair * V7X_LANES, V7X_LANES)
            x2 = xs[:, pl.ds(ch, V7X_LANES)]
            s2 = state[:, pl.ds(ch, V7X_LANES)]
            lhs_y, lhs_s, etot = [], [], []
            for hh in (2 * pair, 2 * pair + 1):
                col = cumg[:, hh:hh + 1]
                row = cum_t[hh:hh + 1, :]
                dtrow = dt_t[hh:hh + 1, :]
                decay = jnp.exp(jnp.where(causal, col - row, neg_inf))
                lhs_y.append((cb * decay * dtrow).astype(BF16))
                lhs_y.append((cg * jnp.exp(col)).astype(BF16))
                tot = cum_t[hh:hh + 1, q - 1:q]
                lhs_s.append((bt * (dtrow * jnp.exp(tot - row))).astype(BF16))
                etot.append(jnp.exp(tot))
            x_lo = jnp.where(lo, x2, 0.0).astype(BF16)
            x_hi = jnp.where(lo, 0.0, x2).astype(BF16)
            s_lo = jnp.where(lo, s2, 0.0).astype(BF16)
            s_hi = jnp.where(lo, 0.0, s2).astype(BF16)
            y2 = jnp.dot(jnp.concatenate(lhs_y, axis=1), jnp.concatenate([x_lo, s_lo, x_hi, s_hi], axis=0),
                         preferred_element_type=F32)
            ybuf[:, pl.ds(ch, V7X_LANES)] = y2 + dfull_ref[:, pl.ds(ch, V7X_LANES)] * x2
            upd = jnp.dot(jnp.concatenate(lhs_s, axis=1), jnp.concatenate([x_lo, x_hi], axis=0),
                          preferred_element_type=F32)
            state[:, pl.ds(ch, V7X_LANES)] = jnp.where(lo, etot[0], etot[1]) * s2 + upd
        return carry

    lax.fori_loop(0, ngroups, group, 0)

    z = z_ref[...].astype(F32)
    yz = ybuf[...] * (z * jax.nn.sigmoid(z))
    var = jnp.mean(yz * yz, axis=-1, keepdims=True)
    o_ref[...] = (yz * lax.rsqrt(var + RMS_EPS) * nw_ref[...]).astype(o_ref.dtype)


def _ssd_core(zx, dt_raw, conv_w, conv_b, dt_bias, a_log, d_skip, norm_w, inner, bc):
    s = zx.shape[0]
    q = SSD_CHUNK
    heads = inner // SSD_HEADDIM
    ngroups = bc // SSD_STATE
    hpg = heads // ngroups
    assert hpg % 2 == 0 and hpg <= V7X_LANES and inner % bc == 0 and s % q == 0
    gl = ngroups * V7X_LANES

    def pad_heads(v):
        v = v.astype(F32).reshape(ngroups, hpg)
        return jnp.pad(v, ((0, 0), (0, V7X_LANES - hpg))).reshape(1, gl)

    cw = conv_w.astype(F32)
    cbias = conv_b.astype(F32).reshape(1, -1)
    dfull = jnp.repeat(d_skip.astype(F32), SSD_HEADDIM).reshape(1, inner)
    const = lambda shape: pl.BlockSpec(shape, lambda c: (0, 0))
    nb = inner // bc
    return pl.pallas_call(
        functools.partial(_ssd_kernel, q=q, ngroups=ngroups, hpg=hpg),
        grid=(s // q,),
        in_specs=[
            pl.BlockSpec((q, inner), lambda c: (c, 0)),
            pl.BlockSpec((q, inner), lambda c: (c, 1)),
            pl.BlockSpec((q, bc), lambda c: (c, 2 * nb)),
            pl.BlockSpec((q, bc), lambda c: (c, 2 * nb + 1)),
            pl.BlockSpec((q, gl), lambda c: (c, 0)),
            const((CONV_W, inner)), const((1, inner)),
            const((CONV_W, bc)), const((1, bc)),
            const((CONV_W, bc)), const((1, bc)),
            const((1, gl)), const((1, gl)), const((1, inner)), const((1, inner)),
        ],
        out_specs=pl.BlockSpec((q, inner), lambda c: (c, 0)),
        out_shape=jax.ShapeDtypeStruct((s, inner), BF16),
        scratch_shapes=[
            pltpu.VMEM((q + HALO, inner), F32), pltpu.VMEM((q + HALO, bc), F32), pltpu.VMEM((q + HALO, bc), F32),
            pltpu.VMEM((q, inner), F32), pltpu.VMEM((q, bc), F32), pltpu.VMEM((q, bc), F32),
            pltpu.VMEM((q, gl), F32), pltpu.VMEM((q, gl), F32),
            pltpu.VMEM((SSD_STATE, inner), F32), pltpu.VMEM((q, inner), F32),
        ],
        compiler_params=_params("arbitrary"),
        name="ssd_core",
    )(zx, zx, zx, zx, dt_raw,
      cw[:, :inner], cbias[:, :inner], cw[:, inner:inner + bc], cbias[:, inner:inner + bc],
      cw[:, inner + bc:], cbias[:, inner + bc:],
      pad_heads(dt_bias), pad_heads(a_log), dfull, norm_w.astype(F32).reshape(1, inner))


def _router_kernel(x_ref, w_ref, wr_ref, xn_ref, meta_ref, cnt_ref, carry, *, tb, ne):
    b = pl.program_id(0)

    @pl.when(b == 0)
    def _():
        carry[...] = jnp.zeros_like(carry)

    x = x_ref[...]
    var = jnp.mean(x * x, axis=-1, keepdims=True)
    xn = x * lax.rsqrt(var + RMS_EPS) * w_ref[...]
    xn_ref[...] = xn
    logits = jnp.dot(xn, wr_ref[...], preferred_element_type=F32, precision=lax.Precision.HIGHEST)
    lane = lax.broadcasted_iota(jnp.int32, (tb, V7X_LANES), 1).astype(F32)
    neg_inf = jnp.float32(-jnp.inf)
    l1 = jnp.where(lane < ne, logits, neg_inf)
    m1 = jnp.max(l1, axis=1, keepdims=True)
    i1 = jnp.min(jnp.where(l1 == m1, lane, float(V7X_LANES)), axis=1, keepdims=True)
    oh1 = lane == i1
    l2 = jnp.where(oh1, neg_inf, l1)
    m2 = jnp.max(l2, axis=1, keepdims=True)
    i2 = jnp.min(jnp.where(l2 == m2, lane, float(V7X_LANES)), axis=1, keepdims=True)
    oh2 = lane == i2
    e2 = jnp.exp(m2 - m1)
    den = 1.0 + e2
    g1 = 1.0 / den
    g2 = e2 / den
    sel = jnp.where(oh1, 1.0, jnp.where(oh2, 1.0, 0.0))
    rr = lax.broadcasted_iota(jnp.int32, (tb, tb), 0)
    cc = lax.broadcasted_iota(jnp.int32, (tb, tb), 1)
    before = jnp.where(rr > cc, 1.0, 0.0).astype(BF16)
    rank = jnp.dot(before, sel.astype(BF16), preferred_element_type=F32) + carry[...]
    r1 = jnp.sum(jnp.where(oh1, rank, 0.0), axis=1, keepdims=True)
    r2 = jnp.sum(jnp.where(oh2, rank, 0.0), axis=1, keepdims=True)
    total = carry[...] + jnp.sum(sel, axis=0, keepdims=True)
    carry[...] = total
    cnt_ref[...] = jnp.broadcast_to(total, cnt_ref.shape)
    meta = jnp.zeros((tb, V7X_LANES), F32)
    for k, v in enumerate((i1, i2, g1, g2, r1, r2)):
        meta = jnp.where(lane == float(k), v, meta)
    meta_ref[...] = meta


def _route(x, norm_w, w_router):
    s, d = x.shape
    ne = w_router.shape[1]
    tb = _tile(s, 256)
    wr = jnp.pad(w_router.astype(F32), ((0, 0), (0, V7X_LANES - ne)))
    return pl.pallas_call(
        functools.partial(_router_kernel, tb=tb, ne=ne),
        grid=(s // tb,),
        in_specs=[
            pl.BlockSpec((tb, d), lambda b: (b, 0)),
            pl.BlockSpec((1, d), lambda b: (0, 0)),
            pl.BlockSpec((d, V7X_LANES), lambda b: (0, 0)),
        ],
        out_specs=[
            pl.BlockSpec((tb, d), lambda b: (b, 0)),
            pl.BlockSpec((tb, V7X_LANES), lambda b: (b, 0)),
            pl.BlockSpec((V7X_SUBLANES, V7X_LANES), lambda b: (0, 0)),
        ],
        out_shape=[
            jax.ShapeDtypeStruct((s, d), F32),
            jax.ShapeDtypeStruct((s, V7X_LANES), F32),
            jax.ShapeDtypeStruct((V7X_SUBLANES, V7X_LANES), F32),
        ],
        scratch_shapes=[pltpu.VMEM((1, V7X_LANES), F32)],
        compiler_params=_params("arbitrary"),
        name="moe_router",
    )(x, norm_w.astype(F32).reshape(1, d), wr)


def _pos_kernel(meta_ref, offs_ref, pos_ref):
    meta = meta_ref[...]
    tb = meta.shape[0]
    lane = lax.broadcasted_iota(jnp.int32, (tb, V7X_LANES), 1).astype(F32)
    offs = offs_ref[...]
    p1 = jnp.sum(jnp.where(lane == meta[:, 0:1], offs, 0.0), axis=1, keepdims=True) + meta[:, 4:5]
    p2 = jnp.sum(jnp.where(lane == meta[:, 1:2], offs, 0.0), axis=1, keepdims=True) + meta[:, 5:6]
    both = jnp.where(lane == 0.0, p1, jnp.where(lane == 1.0, p2, 0.0))
    pos_ref[...] = both.T[0:V7X_SUBLANES, :].astype(jnp.int32)


def _positions(meta, offs):
    s = meta.shape[0]
    tb = _tile(s, 1024)
    return pl.pallas_call(
        _pos_kernel,
        grid=(s // tb,),
        in_specs=[pl.BlockSpec((tb, V7X_LANES), lambda b: (b, 0)), pl.BlockSpec((1, V7X_LANES), lambda b: (0, 0))],
        out_specs=pl.BlockSpec((V7X_SUBLANES, tb), lambda b: (0, b)),
        out_shape=jax.ShapeDtypeStruct((V7X_SUBLANES, s), jnp.int32),
        compiler_params=_params("parallel"),
        name="moe_positions",
    )(meta, offs)


def _row_copy(src, src_row, dst, dst_row, sem):
    return pltpu.make_async_copy(src.at[pl.ds(src_row, 1), :], dst.at[pl.ds(dst_row, 1), :], sem)


def _scatter_kernel(p1_ref, p2_ref, xn_ref, init_ref, xs_ref, sem, *, tb):
    del init_ref

    def issue(r, carry):
        _row_copy(xn_ref, r, xs_ref, p1_ref[0, 0, r], sem).start()
        _row_copy(xn_ref, r, xs_ref, p2_ref[0, 0, r], sem).start()
        return carry

    lax.fori_loop(0, tb, issue, 0)

    def drain(r, carry):
        _row_copy(xn_ref, r, xs_ref, p1_ref[0, 0, r], sem).wait()
        _row_copy(xn_ref, r, xs_ref, p2_ref[0, 0, r], sem).wait()
        return carry

    lax.fori_loop(0, tb, drain, 0)


def _scatter_rows(xn, pos1, pos2, p_rows):
    s, d = xn.shape
    tb = pos1.shape[2]
    smem = lambda: pl.BlockSpec((1, 1, tb), lambda b: (b, 0, 0), memory_space=pltpu.SMEM)
    return pl.pallas_call(
        functools.partial(_scatter_kernel, tb=tb),
        grid=(s // tb,),
        in_specs=[smem(), smem(), pl.BlockSpec((tb, d), lambda b: (b, 0)), pl.BlockSpec(memory_space=pl.ANY)],
        out_specs=pl.BlockSpec(memory_space=pl.ANY),
        out_shape=jax.ShapeDtypeStruct((p_rows, d), xn.dtype),
        scratch_shapes=[pltpu.SemaphoreType.DMA(())],
        input_output_aliases={3: 0},
        compiler_params=_params("arbitrary"),
        name="moe_scatter",
    )(pos1, pos2, xn, jnp.zeros((p_rows, d), xn.dtype))


def _moe_up_kernel(te_ref, nt_ref, xs_ref, wg_ref, wu_ref, o_ref):
    i = pl.program_id(1)

    @pl.when(i < nt_ref[0])
    def _():
        a = xs_ref[...].astype(BF16)
        g = jnp.dot(a, wg_ref[...], preferred_element_type=F32)
        u = jnp.dot(a, wu_ref[...], preferred_element_type=F32)
        o_ref[...] = (g * jax.nn.sigmoid(g) * u).astype(o_ref.dtype)

    @pl.when(i >= nt_ref[0])
    def _():
        o_ref[...] = jnp.zeros_like(o_ref)


def _moe_down_kernel(te_ref, nt_ref, h_ref, wd_ref, o_ref):
    i = pl.program_id(1)

    @pl.when(i < nt_ref[0])
    def _():
        o_ref[...] = jnp.dot(h_ref[...], wd_ref[...], preferred_element_type=F32).astype(o_ref.dtype)

    @pl.when(i >= nt_ref[0])
    def _():
        o_ref[...] = jnp.zeros_like(o_ref)


def _grouped(kernel, tile_expert, num_tiles, a, ws, tm, tn, out_dtype, name):
    p_rows, kdim = a.shape
    n = ws[0].shape[2]
    tn = _tile(n, tn)
    a_map = lambda j, i, te, nt: (jnp.minimum(i, nt[0] - 1), 0)
    w_map = lambda j, i, te, nt: (te[i], 0, j)
    grid_spec = pltpu.PrefetchScalarGridSpec(
        num_scalar_prefetch=2,
        grid=(n // tn, p_rows // tm),
        in_specs=[pl.BlockSpec((tm, kdim), a_map)] + [pl.BlockSpec((None, kdim, tn), w_map) for _ in ws],
        out_specs=pl.BlockSpec((tm, tn), lambda j, i, te, nt: (i, j)),
    )
    return pl.pallas_call(
        kernel,
        grid_spec=grid_spec,
        out_shape=jax.ShapeDtypeStruct((p_rows, n), out_dtype),
        compiler_params=_params("parallel", "arbitrary"),
        name=name,
    )(tile_expert, num_tiles, a, *ws)


def _combine_kernel(p1_ref, p2_ref, meta_ref, x_ref, fw_ref, ys_ref, o_ref, y1, y2, sem, *, tb):
    def issue(r, carry):
        _row_copy(ys_ref, p1_ref[0, 0, r], y1, r, sem).start()
        _row_copy(ys_ref, p2_ref[0, 0, r], y2, r, sem).start()
        return carry

    lax.fori_loop(0, tb, issue, 0)

    def drain(r, carry):
        _row_copy(ys_ref, p1_ref[0, 0, r], y1, r, sem).wait()
        _row_copy(ys_ref, p2_ref[0, 0, r], y2, r, sem).wait()
        return carry

    lax.fori_loop(0, tb, drain, 0)
    meta = meta_ref[...]
    x = x_ref[...] + (meta[:, 2:3] * y1[...] + meta[:, 3:4] * y2[...])
    var = jnp.mean(x * x, axis=-1, keepdims=True)
    o_ref[...] = (x * lax.rsqrt(var + RMS_EPS) * fw_ref[...]).astype(o_ref.dtype)


def _combine(x, ys, meta, pos1, pos2, final_w):
    s, d = x.shape
    tb = pos1.shape[2]
    smem = lambda: pl.BlockSpec((1, 1, tb), lambda b: (b, 0, 0), memory_space=pltpu.SMEM)
    return pl.pallas_call(
        functools.partial(_combine_kernel, tb=tb),
        grid=(s // tb,),
        in_specs=[
            smem(), smem(),
            pl.BlockSpec((tb, V7X_LANES), lambda b: (b, 0)),
            pl.BlockSpec((tb, d), lambda b: (b, 0)),
            pl.BlockSpec((1, d), lambda b: (0, 0)),
            pl.BlockSpec(memory_space=pl.ANY),
        ],
        out_specs=pl.BlockSpec((tb, d), lambda b: (b, 0)),
        out_shape=jax.ShapeDtypeStruct((s, d), x.dtype),
        scratch_shapes=[pltpu.VMEM((tb, d), ys.dtype), pltpu.VMEM((tb, d), ys.dtype), pltpu.SemaphoreType.DMA(())],
        compiler_params=_params("arbitrary"),
        name="moe_combine",
    )(pos1, pos2, meta, x, final_w.astype(F32).reshape(1, d), ys)


def _moe_and_final_norm(x, ffn_norm, w_router, w_gate, w_up, w_down, final_norm):
    s, d = x.shape
    ne = w_router.shape[1]
    tm = min(256, max(V7X_SUBLANES, s // 8))
    tb = _tile(s, 256)
    xn, meta, counts = _route(x, ffn_norm, w_router)

    cnt = counts[0, :ne].astype(jnp.int32)
    tiles = (cnt + tm - 1) // tm
    ends = jnp.cumsum(tiles)
    starts = ends - tiles
    n_tiles = s * TOP_K // tm + ne
    p_rows = n_tiles * tm
    tile_expert = jnp.minimum(
        jnp.sum(jnp.arange(n_tiles, dtype=jnp.int32)[:, None] >= ends[None, :], axis=1), ne - 1).astype(jnp.int32)
    num_tiles = ends[ne - 1:ne].astype(jnp.int32)
    offs = jnp.pad((starts * tm).astype(F32), (0, V7X_LANES - ne)).reshape(1, V7X_LANES)

    pos = _positions(meta, offs)
    pos1 = pos[0].reshape(s // tb, 1, tb)
    pos2 = pos[1].reshape(s // tb, 1, tb)
    xs = _scatter_rows(xn, pos1, pos2, p_rows)
    hs = _grouped(_moe_up_kernel, tile_expert, num_tiles, xs, [w_gate.astype(BF16), w_up.astype(BF16)],
                  tm, MM_TILE, BF16, "moe_up")
    ys = _grouped(_moe_down_kernel, tile_expert, num_tiles, hs, [w_down.astype(BF16)], tm, MM_TILE, F32, "moe_down")
    return _combine(x, ys, meta, pos1, pos2, final_norm)


def kernel(x, l0_mix_norm, l0_lru_w_in, l0_lru_conv_w, l0_lru_conv_b, l0_lru_w_ra, l0_lru_b_ra, l0_lru_w_ri, l0_lru_b_ri, l0_lru_lambda, l0_lru_w_out, l0_ffn_norm, l0_ffn_w_gate, l0_ffn_w_up, l0_ffn_w_down, l1_mix_norm, l1_ssd_w_in, l1_ssd_conv_w, l1_ssd_conv_b, l1_ssd_dt_bias, l1_ssd_a_log, l1_ssd_d, l1_ssd_norm, l1_ssd_w_out, l1_ffn_norm, l1_moe_router, l1_moe_w_gate, l1_moe_w_up, l1_moe_w_down, final_norm):
    bsz, s, d = x.shape
    assert bsz == 1
    x = x.reshape(s, d).astype(F32)
    bf = lambda w: w.astype(BF16)

    xn = _rmsnorm(x, l0_mix_norm, BF16)
    gx = _matmul(xn, bf(l0_lru_w_in), BF16, name="lru_in")
    y = _lru_core(gx, l0_lru_conv_w, l0_lru_conv_b, l0_lru_w_ra, l0_lru_b_ra, l0_lru_w_ri, l0_lru_b_ri, l0_lru_lambda)
    x = _matmul(y, bf(l0_lru_w_out), F32, res=x, name="lru_out")
    xn = _rmsnorm(x, l0_ffn_norm, BF16)
    h = _swiglu_up(xn, bf(l0_ffn_w_gate), bf(l0_ffn_w_up))
    x = _matmul(h, bf(l0_ffn_w_down), F32, res=x, name="ffn_down")

    inner = l1_ssd_norm.shape[0]
    heads = l1_ssd_dt_bias.shape[0]
    bc = (l1_ssd_conv_w.shape[1] - inner) // 2
    ngroups = bc // SSD_STATE
    hpg = heads // ngroups
    nzx = 2 * inner + 2 * bc
    xn = _rmsnorm(x, l1_mix_norm, BF16)
    zx = _matmul(xn, bf(l1_ssd_w_in[:, :nzx]), BF16, name="ssd_in")
    w_dt = l1_ssd_w_in[:, nzx:].reshape(d, ngroups, hpg)
    w_dt = jnp.pad(w_dt, ((0, 0), (0, 0), (0, V7X_LANES - hpg))).reshape(d, ngroups * V7X_LANES)
    dt_raw = _matmul(xn, bf(w_dt), F32, name="ssd_dt")
    y = _ssd_core(zx, dt_raw, l1_ssd_conv_w, l1_ssd_conv_b, l1_ssd_dt_bias, l1_ssd_a_log, l1_ssd_d, l1_ssd_norm,
                  inner, bc)
    x = _matmul(y, bf(l1_ssd_w_out), F32, res=x, name="ssd_out")
    out = _moe_and_final_norm(x, l1_ffn_norm, l1_moe_router, l1_moe_w_gate, l1_moe_w_up, l1_moe_w_down, final_norm)
    return out.reshape(bsz, s, d)
```
